```python
import jax, jax.numpy as jnp
from jax import lax
import numpy as np

D_MODEL = 1024
BATCH = 8
SEQ = 2048
DEPTH = 4
DEC_BATCH = 32
DEC_SEQ = 4
PAST_LEN = 8192
PAGE_SIZE = 128

N_HEADS = 16
HEAD_DIM = D_MODEL // N_HEADS
D_FF = 4 * D_MODEL
CONV_WIDTH = 31
N_A = DEPTH // 2
N_B = DEPTH - N_A
Q_BLOCK = 128
LN_EPS = 1e-5
DEEPNORM_ALPHA = (2.0 * DEPTH) ** 0.25
DEEPNORM_BETA = (8.0 * DEPTH) ** -0.25
BIAS_HI = -4.0
BIAS_LO = -9.0

kernel_name = 'yoco_conformer_stickbreaking_decoder_step'


def layer_norm(x, g, b):
    xf = x.astype(jnp.float32)
    mu = jnp.mean(xf, axis=-1, keepdims=True)
    var = jnp.mean(jnp.square(xf - mu), axis=-1, keepdims=True)
    y = (xf - mu) * lax.rsqrt(var + LN_EPS)
    return (y * g.astype(jnp.float32) + b.astype(jnp.float32)).astype(x.dtype)


def ada_mod(c, w, b):
    m = jax.nn.silu(c) @ w + b
    shift, scale, gate = jnp.split(m[:, None, :], 3, axis=-1)
    return shift, scale, gate


def post_norm_residual(x, y, gate, g, b):
    return layer_norm(DEEPNORM_ALPHA * x + (1.0 + gate) * y, g, b)


def conformer_conv(h, conv_prev, pw1_w, pw1_b, dw_w, dw_b, ln_g, ln_b, pw2_w):
    a, g = jnp.split(h @ pw1_w + pw1_b, 2, axis=-1)
    u = a * jax.nn.sigmoid(g)
    full = jnp.concatenate([conv_prev, u], axis=1)
    y = lax.conv_general_dilated(full, dw_w[:, None, :], (1,), 'VALID',
                                 dimension_numbers=('NWC', 'WIO', 'NWC'),
                                 feature_group_count=D_MODEL) + dw_b
    y = jax.nn.silu(layer_norm(y, ln_g, ln_b))
    return y @ pw2_w, full[:, -(CONV_WIDTH - 1):]


def stick_breaking(q, k, v, bias, q_pos, k_pos):
    z = jnp.einsum('bqhd,bkhd->bhqk', q, k).astype(jnp.float32) * (HEAD_DIM ** -0.5)
    z = z + bias.astype(jnp.float32)[None, :, None, None]
    mask = k_pos[None, :] < q_pos[:, None]
    log_rem = jnp.where(mask, jax.nn.log_sigmoid(-z), 0.0)
    suffix = lax.cumsum(log_rem, axis=3, reverse=True)
    after = jnp.concatenate([suffix[..., 1:], jnp.zeros_like(suffix[..., :1])], axis=-1)
    w = jnp.where(mask, jnp.exp(jax.nn.log_sigmoid(z) + after), 0.0)
    return jnp.einsum('bhqk,bkhd->bqhd', w.astype(v.dtype), v)


def stick_breaking_prompt(q, k, v, bias):
    B, T = q.shape[0], q.shape[1]
    nb = T // Q_BLOCK
    qb = q.reshape(B, nb, Q_BLOCK, N_HEADS, HEAD_DIM).transpose(1, 0, 2, 3, 4)
    pos = jnp.arange(T, dtype=jnp.int32)
    qpos = pos.reshape(nb, Q_BLOCK)
    ob = lax.map(lambda a: stick_breaking(a[0], k, v, bias, a[1], pos), (qb, qpos))
    return ob.transpose(1, 0, 2, 3, 4).reshape(B, T, N_HEADS, HEAD_DIM)


def run_trunk(x, c, conv_prev, k_past, v_past, ada_w, ada_b, ln_g, ln_b, mlp_w1, mlp_w2,
              conv_pw1_w, conv_pw1_b, conv_dw_w, conv_dw_b, conv_ln_g, conv_ln_b, conv_pw2_w,
              w_kv, attn_wq, attn_wo, attn_bias):
    B, T = x.shape[0], x.shape[1]
    conv_new = []
    k_new = v_new = k_all = v_all = None
    for l in range(DEPTH):
        shift, scale, gate = ada_mod(c, ada_w[l, 0], ada_b[l, 0])
        h = x * (1.0 + scale) + shift
        if l < N_A:
            y, st = conformer_conv(h, conv_prev[l], conv_pw1_w[l], conv_pw1_b[l], conv_dw_w[l],
                                   conv_dw_b[l], conv_ln_g[l], conv_ln_b[l], conv_pw2_w[l])
            conv_new.append(st)
        else:
            if l == N_A:
                k_flat, v_flat = jnp.split(x @ w_kv, 2, axis=-1)
                k_new = k_flat.reshape(B, T, N_HEADS, HEAD_DIM)
                v_new = v_flat.reshape(B, T, N_HEADS, HEAD_DIM)
                if k_past is None:
                    k_all, v_all = k_new, v_new
                else:
                    k_all = jnp.concatenate([k_past, k_new], axis=1)
                    v_all = jnp.concatenate([v_past, v_new], axis=1)
            j = l - N_A
            q = (h @ attn_wq[j]).reshape(B, T, N_HEADS, HEAD_DIM)
            if k_past is None:
                o = stick_breaking_prompt(q, k_all, v_all, attn_bias[j])
            else:
                p = k_past.shape[1]
                q_pos = p + jnp.arange(T, dtype=jnp.int32)
                k_pos = jnp.arange(p + T, dtype=jnp.int32)
                o = stick_breaking(q, k_all, v_all, attn_bias[j], q_pos, k_pos)
            y = o.reshape(B, T, D_MODEL) @ attn_wo[j]
        x = post_norm_residual(x, y, gate, ln_g[l, 0], ln_b[l, 0])
        shift, scale, gate = ada_mod(c, ada_w[l, 1], ada_b[l, 1])
        h = x * (1.0 + scale) + shift
        y = jnp.square(jax.nn.relu(h @ mlp_w1[l])) @ mlp_w2[l]
        x = post_norm_residual(x, y, gate, ln_g[l, 1], ln_b[l, 1])
    return x, jnp.stack(conv_new, axis=0), k_new, v_new


def setup_inputs(seed: int = 0) -> dict:
    key = jax.random.key(seed)
    ks = jax.random.split(key, 32)
    n_pages = PAST_LEN // PAGE_SIZE
    n_used = DEC_BATCH * n_pages
    n_pool = (n_used * 5) // 4
    f = jnp.float32
    dsc = D_MODEL ** -0.5
    x_prompt = jax.random.normal(ks[0], (BATCH, SEQ, D_MODEL), f)
    x_sample = jax.random.normal(ks[1], (DEC_BATCH, DEC_SEQ, D_MODEL), f)
    c_prompt = jax.random.normal(ks[2], (BATCH, D_MODEL), f)
    c_sample = jax.random.normal(ks[3], (DEC_BATCH, D_MODEL), f)
    state_conv = 0.5 * jax.random.normal(ks[4], (N_A, DEC_BATCH, CONV_WIDTH - 1, D_MODEL), f)
    cache_k = jax.random.normal(ks[5], (n_pool, PAGE_SIZE, N_HEADS, HEAD_DIM), f)
    cache_v = jax.random.normal(ks[6], (n_pool, PAGE_SIZE, N_HEADS, HEAD_DIM), f)
    page_table = jax.random.permutation(ks[7], n_pool)[:n_used].reshape(DEC_BATCH, n_pages).astype(jnp.int32)
    ada_w = jax.random.normal(ks[8], (DEPTH, 2, D_MODEL, 3 * D_MODEL), f) * dsc
    ada_b = 0.02 * jax.random.normal(ks[9], (DEPTH, 2, 3 * D_MODEL), f)
    ln_g = 1.0 + 0.02 * jax.random.normal(ks[10], (DEPTH, 2, D_MODEL), f)
    ln_b = 0.02 * jax.random.normal(ks[11], (DEPTH, 2, D_MODEL), f)
    mlp_w1 = jax.random.normal(ks[12], (DEPTH, D_MODEL, D_FF), f) * dsc
    mlp_w2 = jax.random.normal(ks[13], (DEPTH, D_FF, D_MODEL), f) * (D_FF ** -0.5) * DEEPNORM_BETA
    conv_pw1_w = jax.random.normal(ks[14], (N_A, D_MODEL, 2 * D_MODEL), f) * dsc
    conv_pw1_b = 0.02 * jax.random.normal(ks[15], (N_A, 2 * D_MODEL), f)
    conv_dw_w = jax.random.normal(ks[16], (N_A, CONV_WIDTH, D_MODEL), f) * (CONV_WIDTH ** -0.5)
    conv_dw_b = 0.02 * jax.random.normal(ks[17], (N_A, D_MODEL), f)
    conv_ln_g = 1.0 + 0.02 * jax.random.normal(ks[18], (N_A, D_MODEL), f)
    conv_ln_b = 0.02 * jax.random.normal(ks[19], (N_A, D_MODEL), f)
    conv_pw2_w = jax.random.normal(ks[20], (N_A, D_MODEL, D_MODEL), f) * dsc * DEEPNORM_BETA
    w_kv = jax.random.normal(ks[21], (D_MODEL, 2 * D_MODEL), f) * dsc
    attn_wq = jax.random.normal(ks[22], (N_B, D_MODEL, D_MODEL), f) * dsc
    attn_wo = jax.random.normal(ks[23], (N_B, D_MODEL, D_MODEL), f) * dsc * DEEPNORM_BETA
    attn_bias = BIAS_LO + (BIAS_HI - BIAS_LO) * jax.random.uniform(ks[24], (N_B, N_HEADS), f)
    return {'x_prompt': x_prompt, 'x_sample': x_sample, 'c_prompt': c_prompt, 'c_sample': c_sample,
            'state_conv': state_conv, 'cache_k': cache_k, 'cache_v': cache_v, 'page_table': page_table,
            'ada_w': ada_w, 'ada_b': ada_b, 'ln_g': ln_g, 'ln_b': ln_b, 'mlp_w1': mlp_w1, 'mlp_w2': mlp_w2,
            'conv_pw1_w': conv_pw1_w, 'conv_pw1_b': conv_pw1_b, 'conv_dw_w': conv_dw_w, 'conv_dw_b': conv_dw_b,
            'conv_ln_g': conv_ln_g, 'conv_ln_b': conv_ln_b, 'conv_pw2_w': conv_pw2_w,
            'w_kv': w_kv, 'attn_wq': attn_wq, 'attn_wo': attn_wo, 'attn_bias': attn_bias}


def reference(x_prompt, x_sample, c_prompt, c_sample, state_conv, cache_k, cache_v, page_table,
              ada_w, ada_b, ln_g, ln_b, mlp_w1, mlp_w2, conv_pw1_w, conv_pw1_b, conv_dw_w, conv_dw_b,
              conv_ln_g, conv_ln_b, conv_pw2_w, w_kv, attn_wq, attn_wo, attn_bias):
    conv_zero = jnp.zeros((N_A, x_prompt.shape[0], CONV_WIDTH - 1, D_MODEL), x_prompt.dtype)
    y_prompt, new_conv_prompt, new_k_prompt, new_v_prompt = run_trunk(
        x_prompt, c_prompt, conv_zero, None, None, ada_w, ada_b, ln_g, ln_b, mlp_w1, mlp_w2,
        conv_pw1_w, conv_pw1_b, conv_dw_w, conv_dw_b, conv_ln_g, conv_ln_b, conv_pw2_w,
        w_kv, attn_wq, attn_wo, attn_bias)
    db, n_pages = page_table.shape
    past_len = n_pages * cache_k.shape[1]
    k_past = cache_k[page_table].reshape(db, past_len, N_HEADS, HEAD_DIM)
    v_past = cache_v[page_table].reshape(db, past_len, N_HEADS, HEAD_DIM)
    y_sample, new_conv_sample, new_k_sample, new_v_sample = run_trunk(
        x_sample, c_sample, state_conv, k_past, v_past, ada_w, ada_b, ln_g, ln_b, mlp_w1, mlp_w2,
        conv_pw1_w, conv_pw1_b, conv_dw_w, conv_dw_b, conv_ln_g, conv_ln_b, conv_pw2_w,
        w_kv, attn_wq, attn_wo, attn_bias)
    return (y_prompt, y_sample, new_conv_prompt, new_conv_sample, new_k_prompt, new_v_prompt, new_k_sample, new_v_sample)
```

```python
import functools

import jax
import jax.numpy as jnp
from jax import lax
from jax.experimental import pallas as pl
from jax.experimental.pallas import tpu as pltpu

F32 = jnp.float32
BF16 = jnp.bfloat16
LN_EPS = 1e-5

V7X_LANES = 128
V7X_SUBLANES = 8
V7X_MXU_DIM = 256
V7X_VMEM_BYTES = 64 * 1024 * 1024
VMEM_LIMIT = 56 * 1024 * 1024


def _cparams(semantics):
    return pltpu.CompilerParams(dimension_semantics=semantics, vmem_limit_bytes=VMEM_LIMIT)


def _sigmoid(v):
    return 1.0 / (1.0 + jnp.exp(-v))


def _softplus(z):
    return jnp.maximum(z, 0.0) + jnp.log(1.0 + jnp.exp(-jnp.abs(z)))


def _layer_norm(v, g, b):
    mu = jnp.mean(v, axis=-1, keepdims=True)
    d = v - mu
    var = jnp.mean(d * d, axis=-1, keepdims=True)
    return d * lax.rsqrt(var + LN_EPS) * g + b


def _dot(a, b):
    return jnp.dot(a, b, preferred_element_type=F32)


def _dot_nt(a, b):
    return lax.dot_general(a, b, (((1,), (1,)), ((), ())), preferred_element_type=F32)


def _ada_kernel(c_ref, w_ref, b_ref, o_ref):
    c = c_ref[...]
    s = (c * _sigmoid(c)).astype(BF16)
    o_ref[...] = _dot(s, w_ref[...].astype(BF16)) + b_ref[...]


def _ada_call(c, w, b):
    n_sub, d, d3 = w.shape
    tn = d
    rows = c.shape[0]
    return pl.pallas_call(
        _ada_kernel,
        grid=(n_sub, d3 // tn),
        in_specs=[
            pl.BlockSpec((rows, d), lambda i, j: (0, 0)),
            pl.BlockSpec((None, d, tn), lambda i, j: (i, 0, j)),
            pl.BlockSpec((None, 1, tn), lambda i, j: (i, 0, j)),
        ],
        out_specs=pl.BlockSpec((None, rows, tn), lambda i, j: (i, 0, j)),
        out_shape=jax.ShapeDtypeStruct((n_sub, rows, d3), F32),
        compiler_params=_cparams(("parallel", "parallel")),
        name="ada_mod",
    )(c, w, b)


def _mod_spec(mod, tm, rows):
    g, r, d3 = mod.shape
    tiles_per_group = rows // tm // g
    return pl.BlockSpec((None, r, d3), lambda i, *_: (i // tiles_per_group, 0, 0))


def _mlp_kernel(x_ref, mod_ref, w1_ref, w2_ref, g_ref, b_ref, o_ref, h_sc, acc_sc, *, alpha):
    f = pl.program_id(1)
    d = x_ref.shape[-1]

    @pl.when(f == 0)
    def _():
        mod = mod_ref[...]
        h_sc[...] = (x_ref[...] * (1.0 + mod[:, d:2 * d]) + mod[:, :d]).astype(BF16)

    a = jnp.maximum(_dot(h_sc[...], w1_ref[...]), 0.0)
    y = _dot((a * a).astype(BF16), w2_ref[...])

    @pl.when(f == 0)
    def _():
        acc_sc[...] = y

    @pl.when(f > 0)
    def _():
        acc_sc[...] += y

    @pl.when(f == pl.num_programs(1) - 1)
    def _():
        gate = mod_ref[...][:, 2 * d:]
        v = alpha * x_ref[...] + (1.0 + gate) * acc_sc[...]
        o_ref[...] = _layer_norm(v, g_ref[...], b_ref[...])


def _mlp_call(x, mod, w1, w2, g, b, *, alpha, tm, tf):
    rows, d = x.shape
    dff = w1.shape[1]
    return pl.pallas_call(
        functools.partial(_mlp_kernel, alpha=alpha),
        grid=(rows // tm, dff // tf),
        in_specs=[
            pl.BlockSpec((tm, d), lambda i, f: (i, 0)),
            _mod_spec(mod, tm, rows),
            pl.BlockSpec((d, tf), lambda i, f: (0, f)),
            pl.BlockSpec((tf, d), lambda i, f: (f, 0)),
            pl.BlockSpec((1, d), lambda i, f: (0, 0)),
            pl.BlockSpec((1, d), lambda i, f: (0, 0)),
        ],
        out_specs=pl.BlockSpec((tm, d), lambda i, f: (i, 0)),
        out_shape=jax.ShapeDtypeStruct((rows, d), F32),
        scratch_shapes=[pltpu.VMEM((tm, d), BF16), pltpu.VMEM((tm, d), F32)],
        compiler_params=_cparams(("parallel", "arbitrary")),
        name="mlp_sublayer",
    )(x, mod, w1, w2, g, b)


def _kv_kernel(x_ref, w_ref, k_ref, v_ref, kb_ref, vb_ref):
    d = x_ref.shape[-1]
    kv = _dot(x_ref[...].astype(BF16), w_ref[...])
    k = kv[:, :d]
    v = kv[:, d:]
    k_ref[...] = k
    v_ref[...] = v
    kb_ref[...] = k.astype(BF16)
    vb_ref[...] = v.astype(BF16)


def _kv_call(x, w, *, tm):
    rows, d = x.shape
    row_spec = pl.BlockSpec((tm, d), lambda i: (i, 0))
    return pl.pallas_call(
        _kv_kernel,
        grid=(rows // tm,),
        in_specs=[row_spec, pl.BlockSpec((d, 2 * d), lambda i: (0, 0))],
        out_specs=[row_spec] * 4,
        out_shape=[jax.ShapeDtypeStruct((rows, d), F32)] * 2 + [jax.ShapeDtypeStruct((rows, d), BF16)] * 2,
        compiler_params=_cparams(("parallel",)),
        name="kv_proj",
    )(x, w)


def _q_kernel(x_ref, mod_ref, w_ref, q_ref, *, qscale):
    d = x_ref.shape[-1]
    mod = mod_ref[...]
    h = (x_ref[...] * (1.0 + mod[:, d:2 * d]) + mod[:, :d]).astype(BF16)
    q_ref[...] = (_dot(h, w_ref[...]) * qscale).astype(BF16)


def _q_call(x, mod, w, *, qscale, tm):
    rows, d = x.shape
    row_spec = pl.BlockSpec((tm, d), lambda i: (i, 0))
    return pl.pallas_call(
        functools.partial(_q_kernel, qscale=qscale),
        grid=(rows // tm,),
        in_specs=[row_spec, _mod_spec(mod, tm, rows), pl.BlockSpec((d, d), lambda i: (0, 0))],
        out_specs=row_spec,
        out_shape=jax.ShapeDtypeStruct((rows, d), BF16),
        compiler_params=_cparams(("parallel",)),
        name="q_proj",
    )(x, mod, w)


def _oproj_kernel(o_ref, x_ref, mod_ref, w_ref, g_ref, b_ref, y_ref, *, alpha):
    d = x_ref.shape[-1]
    gate = mod_ref[...][:, 2 * d:]
    y = _dot(o_ref[...], w_ref[...])
    v = alpha * x_ref[...] + (1.0 + gate) * y
    y_ref[...] = _layer_norm(v, g_ref[...], b_ref[...])


def _oproj_call(o, x, mod, w, g, b, *, alpha, tm):
    rows, d = x.shape
    row_spec = pl.BlockSpec((tm, d), lambda i: (i, 0))
    vec_spec = pl.BlockSpec((1, d), lambda i: (0, 0))
    return pl.pallas_call(
        functools.partial(_oproj_kernel, alpha=alpha),
        grid=(rows // tm,),
        in_specs=[row_spec, row_spec, _mod_spec(mod, tm, rows), pl.BlockSpec((d, d), lambda i: (0, 0)),
                  vec_spec, vec_spec],
        out_specs=row_spec,
        out_shape=jax.ShapeDtypeStruct((rows, d), F32),
        compiler_params=_cparams(("parallel",)),
        name="attn_out_proj",
    )(o, x, mod, w, g, b)


CONV_ROW_CHUNK = 64


def _conv_prompt_kernel(x_ref, mod_ref, pw1_ref, pw1b_ref, dww_ref, dwb_ref, cg_ref, cb_ref, pw2_ref,
                        g_ref, b_ref, o_ref, st_ref, ubuf, ybuf, *, alpha, width, halo):
    t = pl.program_id(1)
    tm, d = x_ref.shape

    @pl.when(t == 0)
    def _():
        ubuf[0:halo, :] = jnp.zeros((halo, d), F32)

    @pl.when(t > 0)
    def _():
        ubuf[0:halo, :] = ubuf[tm:tm + halo, :]

    x = x_ref[...]
    mod = mod_ref[...]
    h = (x * (1.0 + mod[:, d:2 * d]) + mod[:, :d]).astype(BF16)
    ag = _dot(h, pw1_ref[...]) + pw1b_ref[...]
    ubuf[halo:halo + tm, :] = ag[:, :d] * _sigmoid(ag[:, d:])

    base = halo - (width - 1)
    for r0 in range(0, tm, CONV_ROW_CHUNK):
        for c0 in range(0, d, V7X_LANES):
            acc = jnp.broadcast_to(dwb_ref[:, c0:c0 + V7X_LANES], (CONV_ROW_CHUNK, V7X_LANES))
            for j in range(width):
                rows = pl.ds(base + j + r0, CONV_ROW_CHUNK)
                acc = acc + ubuf[rows, c0:c0 + V7X_LANES] * dww_ref[j:j + 1, c0:c0 + V7X_LANES]
            ybuf[r0:r0 + CONV_ROW_CHUNK, c0:c0 + V7X_LANES] = acc

    y = _layer_norm(ybuf[...], cg_ref[...], cb_ref[...])
    y = (y * _sigmoid(y)).astype(BF16)
    y2 = _dot(y, pw2_ref[...])
    v = alpha * x + (1.0 + mod[:, 2 * d:]) * y2
    o_ref[...] = _layer_norm(v, g_ref[...], b_ref[...])

    @pl.when(t == pl.num_programs(1) - 1)
    def _():
        st_ref[...] = ubuf[tm:tm + halo, :]


def _conv_prompt_call(x, mod, pw1, pw1b, dww, dwb, cg, cb, pw2, g, b, *, alpha, batch, tm):
    rows, d = x.shape
    width = dww.shape[0]
    halo = -(-(width - 1) // V7X_SUBLANES) * V7X_SUBLANES
    nt = rows // batch // tm
    vec = lambda n: pl.BlockSpec((1, n), lambda bi, t: (0, 0))
    out, st = pl.pallas_call(
        functools.partial(_conv_prompt_kernel, alpha=alpha, width=width, halo=halo),
        grid=(batch, nt),
        in_specs=[
            pl.BlockSpec((tm, d), lambda bi, t: (bi * nt + t, 0)),
            pl.BlockSpec((None, 1, 3 * d), lambda bi, t: (bi, 0, 0)),
            pl.BlockSpec((d, 2 * d), lambda bi, t: (0, 0)),
            vec(2 * d),
            pl.BlockSpec((width, d), lambda bi, t: (0, 0)),
            vec(d), vec(d), vec(d),
            pl.BlockSpec((d, d), lambda bi, t: (0, 0)),
            vec(d), vec(d),
        ],
        out_specs=[
            pl.BlockSpec((tm, d), lambda bi, t: (bi * nt + t, 0)),
            pl.BlockSpec((None, halo, d), lambda bi, t: (bi, 0, 0)),
        ],
        out_shape=[jax.ShapeDtypeStruct((rows, d), F32), jax.ShapeDtypeStruct((batch, halo, d), F32)],
        scratch_shapes=[pltpu.VMEM((tm + halo, d), F32), pltpu.VMEM((tm, d), F32)],
        compiler_params=_cparams(("arbitrary", "arbitrary")),
        name="conformer_conv_prompt",
    )(x, mod, pw1, pw1b, dww, dwb, cg, cb, pw2, g, b)
    return out, st[:, halo - (width - 1):, :]


CONV_LANE_CHUNK = 512


def _conv_sample_kernel(x_ref, mod_ref, st_ref, pw1_ref, pw1b_ref, dww_ref, dwb_ref, cg_ref, cb_ref, pw2_ref,
                        g_ref, b_ref, o_ref, nst_ref, ubuf, ybuf, *, alpha):
    rows, d = x_ref.shape
    hist, db, _ = st_ref.shape
    ts = rows // db
    width = hist + 1

    x = x_ref[...]
    mod = mod_ref[...]
    h = (x * (1.0 + mod[:, d:2 * d]) + mod[:, :d]).astype(BF16)
    ag = _dot(h, pw1_ref[...]) + pw1b_ref[...]
    ubuf[...] = ag[:, :d] * _sigmoid(ag[:, d:])

    def full_row(k, c0):
        if k < hist:
            return st_ref[k, :, c0:c0 + CONV_LANE_CHUNK]
        return ubuf[(k - hist) * db:(k - hist + 1) * db, c0:c0 + CONV_LANE_CHUNK]

    for t in range(ts):
        for c0 in range(0, d, CONV_LANE_CHUNK):
            acc = jnp.broadcast_to(dwb_ref[:, c0:c0 + CONV_LANE_CHUNK], (db, CONV_LANE_CHUNK))
            for j in range(width):
                acc = acc + full_row(t + j, c0) * dww_ref[j:j + 1, c0:c0 + CONV_LANE_CHUNK]
            ybuf[t * db:(t + 1) * db, c0:c0 + CONV_LANE_CHUNK] = acc

    y = _layer_norm(ybuf[...], cg_ref[...], cb_ref[...])
    y = (y * _sigmoid(y)).astype(BF16)
    y2 = _dot(y, pw2_ref[...])
    v = alpha * x + (1.0 + mod[:, 2 * d:]) * y2
    o_ref[...] = _layer_norm(v, g_ref[...], b_ref[...])

    for k in range(hist):
        src = k + ts
        if src < hist:
            nst_ref[k] = st_ref[src]
        else:
            nst_ref[k] = ubuf[(src - hist) * db:(src - hist + 1) * db, :]


def _conv_sample_call(x, mod, st, pw1, pw1b, dww, dwb, cg, cb, pw2, g, b, *, alpha):
    rows, d = x.shape
    full = lambda a: pl.BlockSpec(a.shape, lambda i, _n=a.ndim: (0,) * _n)
    args = (x, mod, st, pw1, pw1b, dww, dwb, cg, cb, pw2, g, b)
    return pl.pallas_call(
        functools.partial(_conv_sample_kernel, alpha=alpha),
        grid=(1,),
        in_specs=[full(a) for a in args],
        out_specs=[pl.BlockSpec((rows, d), lambda i: (0, 0)), pl.BlockSpec(st.shape, lambda i: (0, 0, 0))],
        out_shape=[jax.ShapeDtypeStruct((rows, d), F32), jax.ShapeDtypeStruct(st.shape, F32)],
        scratch_shapes=[pltpu.VMEM((rows, d), F32), pltpu.VMEM((rows, d), F32)],
        compiler_params=_cparams(("arbitrary",)),
        name="conformer_conv_sample",
    )(*args)


def _attn_prompt_kernel(bias_ref, q_ref, k_ref, v_ref, o_ref, *, tq, tk, hd):
    p = pl.program_id(1)
    i = pl.program_id(2)
    lanes = q_ref.shape[-1]
    heads = lanes // hd

    q = q_ref[...].astype(F32)
    lane = lax.broadcasted_iota(jnp.int32, (1, lanes), 1)
    qh = [jnp.where(lane // hd == a, q, 0.0).astype(BF16) for a in range(heads)]
    bias = [bias_ref[p * heads + a] for a in range(heads)]

    tri = (lax.broadcasted_iota(jnp.int32, (tk, tk), 0) > lax.broadcasted_iota(jnp.int32, (tk, tk), 1)).astype(BF16)

    def block(k0, carry, masked):
        kblk = k_ref[pl.ds(k0, tk), :]
        vblk = v_ref[pl.ds(k0, tk), :]
        if masked:
            qpos = i * tq + lax.broadcasted_iota(jnp.int32, (tq, tk), 0)
            kpos = k0 + lax.broadcasted_iota(jnp.int32, (tq, tk), 1)
            vis = kpos < qpos
        new = []
        for a in range(heads):
            acc, rsum = carry[a]
            z = _dot_nt(qh[a], kblk) + bias[a]
            sp = _softplus(z)
            spm = jnp.where(vis, sp, 0.0) if masked else sp
            cs = _dot(spm.astype(BF16), tri)
            w = jnp.exp(z - sp - cs - rsum)
            if masked:
                w = jnp.where(vis, w, 0.0)
            acc = acc + _dot(w.astype(BF16), vblk)
            rsum = rsum + jnp.sum(spm, axis=-1, keepdims=True)
            new.append((acc, rsum))
        return tuple(new)

    carry = tuple((jnp.zeros((tq, lanes), F32), jnp.zeros((tq, 1), F32)) for _ in range(heads))
    n_diag = tq // tk
    for dblk in reversed(range(n_diag)):
        carry = block(pl.multiple_of(i * tq + dblk * tk, tk), carry, True)

    n_past = i * n_diag

    def body(n, c):
        kb = n_past - 1 - n
        return block(pl.multiple_of(kb * tk, tk), c, False)

    carry = lax.fori_loop(0, n_past, body, carry)

    out = carry[0][0]
    for a in range(1, heads):
        out = jnp.where(lane >= a * hd, carry[a][0], out)
    o_ref[...] = out.astype(o_ref.dtype)


def _attn_prompt_call(q, k, v, bias, *, batch, hd, tq, tk):
    rows, d = q.shape
    t = rows // batch
    nq = t // tq
    ncol = d // V7X_LANES
    return pl.pallas_call(
        functools.partial(_attn_prompt_kernel, tq=tq, tk=tk, hd=hd),
        grid=(batch, ncol, nq),
        in_specs=[
            pl.BlockSpec(memory_space=pltpu.SMEM),
            pl.BlockSpec((tq, V7X_LANES), lambda b, p, i: (b * nq + i, p)),
            pl.BlockSpec((t, V7X_LANES), lambda b, p, i: (b, p)),
            pl.BlockSpec((t, V7X_LANES), lambda b, p, i: (b, p)),
        ],
        out_specs=pl.BlockSpec((tq, V7X_LANES), lambda b, p, i: (b * nq + i, p)),
        out_shape=jax.ShapeDtypeStruct((rows, d), BF16),
        compiler_params=_cparams(("parallel", "parallel", "parallel")),
        name="stickbreak_attn_prompt",
    )(bias, q, k, v)


def _attn_sample_kernel(pt_ref, q_ref, kn_ref, vn_ref, bias_ref, *refs, pps, n_heads):
    del pt_ref
    k_refs = refs[:pps]
    v_refs = refs[pps:2 * pps]
    o_ref, wt_sc, acc_sc, r_sc = refs[2 * pps:]
    s = pl.program_id(1)
    ts, d = q_ref.shape
    hd = d // n_heads
    nr = ts * n_heads
    page = k_refs[0].shape[0]

    row_head = lax.broadcasted_iota(jnp.int32, (n_heads, d), 0)
    lane_head = lax.broadcasted_iota(jnp.int32, (n_heads, d), 1) // hd
    own = row_head == lane_head

    @pl.when(s == 0)
    def _():
        q = q_ref[...].astype(F32)
        for t in range(ts):
            wt_sc[t * n_heads:(t + 1) * n_heads, :] = jnp.where(
                own, jnp.broadcast_to(q[t:t + 1, :], (n_heads, d)), 0.0).astype(BF16)
        wt = wt_sc[...].astype(F32)
        kn = kn_ref[...]
        vn = vn_ref[...]
        bias_col = bias_ref[:, 0:1]
        row_t = lax.broadcasted_iota(jnp.int32, (nr, 1), 0) // n_heads
        zs, sps, spms = [], [], []
        for j in range(ts):
            z = jnp.sum(wt * kn[j:j + 1, :], axis=-1, keepdims=True) + bias_col
            sp = _softplus(z)
            zs.append(z)
            sps.append(sp)
            spms.append(jnp.where(j < row_t, sp, 0.0))
        acc = jnp.zeros((nr, d), F32)
        after = jnp.zeros((nr, 1), F32)
        for j in reversed(range(ts)):
            w = jnp.where(j < row_t, jnp.exp(zs[j] - sps[j] - after), 0.0)
            acc = acc + w * vn[j:j + 1, :]
            after = after + spms[j]
        acc_sc[...] = acc
        r_sc[...] = jnp.broadcast_to(after, r_sc.shape)

    wt = wt_sc[...]
    bias = bias_ref[...]
    tri = (lax.broadcasted_iota(jnp.int32, (page, page), 0)
           > lax.broadcasted_iota(jnp.int32, (page, page), 1)).astype(BF16)
    acc = acc_sc[...]
    rsum = r_sc[...]
    for i in range(pps):
        kp = k_refs[i][...].astype(BF16)
        vp = v_refs[i][...].astype(BF16)
        z = _dot_nt(wt, kp) + bias
        sp = _softplus(z)
        cs = _dot(sp.astype(BF16), tri)
        w = jnp.exp(z - sp - cs - rsum)
        acc = acc + _dot(w.astype(BF16), vp)
        rsum = rsum + jnp.sum(sp, axis=-1, keepdims=True)
    acc_sc[...] = acc
    r_sc[...] = rsum

    @pl.when(s == pl.num_programs(1) - 1)
    def _():
        for t in range(ts):
            blk = jnp.where(own, acc[t * n_heads:(t + 1) * n_heads, :], 0.0)
            o_ref[t:t + 1, :] = jnp.sum(blk, axis=0, keepdims=True).astype(o_ref.dtype)


def _attn_sample_call(q, kn, vn, cache_k, cache_v, page_table, bias, *, pps):
    db, ts, d = q.shape
    n_pool, page, n_heads, hd = cache_k.shape
    n_pages = page_table.shape[1]
    n_steps = n_pages // pps
    nr = ts * n_heads
    ck = cache_k.reshape(n_pool, page, d)
    cv = cache_v.reshape(n_pool, page, d)
    bias_rows = jnp.broadcast_to(jnp.tile(bias, ts)[:, None], (nr, V7X_LANES)).astype(F32)

    def page_spec(i):
        return pl.BlockSpec((None, page, d), lambda b, s, pt: (pt[b, n_pages - 1 - (s * pps + i)], 0, 0))

    seq_spec = pl.BlockSpec((None, ts, d), lambda b, s, pt: (b, 0, 0))
    grid_spec = pltpu.PrefetchScalarGridSpec(
        num_scalar_prefetch=1,
        grid=(db, n_steps),
        in_specs=[seq_spec, seq_spec, seq_spec, pl.BlockSpec((nr, V7X_LANES), lambda b, s, pt: (0, 0))]
        + [page_spec(i) for i in range(pps)] * 2,
        out_specs=seq_spec,
        scratch_shapes=[pltpu.VMEM((nr, d), BF16), pltpu.VMEM((nr, d), F32), pltpu.VMEM((nr, V7X_LANES), F32)],
    )
    return pl.pallas_call(
        functools.partial(_attn_sample_kernel, pps=pps, n_heads=n_heads),
        grid_spec=grid_spec,
        out_shape=jax.ShapeDtypeStruct((db, ts, d), BF16),
        compiler_params=_cparams(("parallel", "arbitrary")),
        name="stickbreak_attn_sample",
    )(page_table, q, kn, vn, bias_rows, *([ck] * pps), *([cv] * pps))


def _row_tile(rows, want):
    tm = min(rows, want)
    assert rows % tm == 0, (rows, tm)
    return tm


def kernel(x_prompt, x_sample, c_prompt, c_sample, state_conv, cache_k, cache_v, page_table, ada_w, ada_b, ln_g,
           ln_b, mlp_w1, mlp_w2, conv_pw1_w, conv_pw1_b, conv_dw_w, conv_dw_b, conv_ln_g, conv_ln_b, conv_pw2_w,
           w_kv, attn_wq, attn_wo, attn_bias):
    bsz, seq, d = x_prompt.shape
    db, ts, _ = x_sample.shape
    depth = ada_w.shape[0]
    n_a = conv_pw1_w.shape[0]
    n_heads = attn_bias.shape[1]
    hd = d // n_heads
    alpha = (2.0 * depth) ** 0.25
    qscale = hd ** -0.5

    w1 = mlp_w1.astype(BF16)
    w2 = mlp_w2.astype(BF16)
    pw1 = conv_pw1_w.astype(BF16)
    pw2 = conv_pw2_w.astype(BF16)
    wkv = w_kv.astype(BF16)
    wq = attn_wq.astype(BF16)
    wo = attn_wo.astype(BF16)

    c_all = jnp.concatenate([c_prompt, c_sample], axis=0)
    mods = _ada_call(c_all, ada_w.reshape(2 * depth, d, 3 * d), ada_b.reshape(2 * depth, 1, 3 * d))
    mod_p = mods[:, :bsz, None, :]
    mod_s = mods[:, bsz:, :]
    mod_s_tm = jnp.tile(mod_s, (1, ts, 1))[:, None]
    mod_s_bm = jnp.repeat(mod_s, ts, axis=1)[:, None]

    rows_p = bsz * seq
    rows_s = db * ts
    tm_p = _row_tile(seq, 512)
    vec = lambda a: a.reshape(1, -1)

    def trunk_tail(x, mod, l, rows, tm):
        return _mlp_call(x, mod[2 * l + 1], w1[l], w2[l], vec(ln_g[l, 1]), vec(ln_b[l, 1]),
                         alpha=alpha, tm=tm, tf=_row_tile(w1.shape[2], 1024))

    xp = x_prompt.reshape(rows_p, d)
    conv_new_p = []
    for l in range(depth):
        if l < n_a:
            xp, st = _conv_prompt_call(
                xp, mod_p[2 * l], pw1[l], vec(conv_pw1_b[l]), conv_dw_w[l], vec(conv_dw_b[l]),
                vec(conv_ln_g[l]), vec(conv_ln_b[l]), pw2[l], vec(ln_g[l, 0]), vec(ln_b[l, 0]),
                alpha=alpha, batch=bsz, tm=_row_tile(seq, 256))
            conv_new_p.append(st)
        else:
            j = l - n_a
            if l == n_a:
                kp_f, vp_f, kp_b, vp_b = _kv_call(xp, wkv, tm=tm_p)
            q = _q_call(xp, mod_p[2 * l], wq[j], qscale=qscale, tm=tm_p)
            o = _attn_prompt_call(q, kp_b, vp_b, attn_bias[j], batch=bsz, hd=hd,
                                  tq=_row_tile(seq, V7X_MXU_DIM), tk=_row_tile(seq, V7X_MXU_DIM))
            xp = _oproj_call(o, xp, mod_p[2 * l], wo[j], vec(ln_g[l, 0]), vec(ln_b[l, 0]), alpha=alpha, tm=tm_p)
        xp = trunk_tail(xp, mod_p, l, rows_p, tm_p)

    assert n_a >= 1, "the sample group enters its conv layers in time-major row order"
    xs = x_sample.transpose(1, 0, 2).reshape(rows_s, d)
    conv_new_s = []
    for l in range(depth):
        if l < n_a:
            xs, st = _conv_sample_call(
                xs, mod_s_tm[2 * l, 0], state_conv[l].transpose(1, 0, 2), pw1[l], vec(conv_pw1_b[l]), conv_dw_w[l],
                vec(conv_dw_b[l]), vec(conv_ln_g[l]), vec(conv_ln_b[l]), pw2[l], vec(ln_g[l, 0]), vec(ln_b[l, 0]),
                alpha=alpha)
            conv_new_s.append(st.transpose(1, 0, 2))
            xs = trunk_tail(xs, mod_s_tm, l, rows_s, rows_s)
            if l == n_a - 1:
                xs = xs.reshape(ts, db, d).transpose(1, 0, 2).reshape(rows_s, d)
        else:
            j = l - n_a
            if l == n_a:
                ks_f, vs_f, _, _ = _kv_call(xs, wkv, tm=rows_s)
            q = _q_call(xs, mod_s_bm[2 * l], wq[j], qscale=qscale, tm=rows_s)
            o = _attn_sample_call(q.reshape(db, ts, d), ks_f.reshape(db, ts, d), vs_f.reshape(db, ts, d),
                                  cache_k, cache_v, page_table, attn_bias[j], pps=8)
            xs = _oproj_call(o.reshape(rows_s, d), xs, mod_s_bm[2 * l], wo[j], vec(ln_g[l, 0]), vec(ln_b[l, 0]),
                             alpha=alpha, tm=rows_s)
            xs = trunk_tail(xs, mod_s_bm, l, rows_s, rows_s)

    return (xp.reshape(bsz, seq, d), xs.reshape(db, ts, d),
            jnp.stack(conv_new_p, axis=0), jnp.stack(conv_new_s, axis=0),
            kp_f.reshape(bsz, seq, n_heads, hd), vp_f.reshape(bsz, seq, n_heads, hd),
            ks_f.reshape(db, ts, n_heads, hd), vs_f.reshape(db, ts, n_heads, hd))
```

```python
import functools

import jax
import jax.numpy as jnp
from jax import lax
from jax.experimental import pallas as pl
from jax.experimental.pallas import tpu as pltpu

F32 = jnp.float32
BF16 = jnp.bfloat16
LN_EPS = 1e-5

V7X_LANES = 128
V7X_SUBLANES = 8
V7X_MXU_DIM = 256
V7X_VMEM_BYTES = 64 * 1024 * 1024
VMEM_LIMIT = 56 * 1024 * 1024


def _cparams(semantics):
    return pltpu.CompilerParams(dimension_semantics=semantics, vmem_limit_bytes=VMEM_LIMIT)


def _sigmoid(v):
    return 1.0 / (1.0 + jnp.exp(-v))


LOG2E = 1.4426950408889634


def _softplus2(z2):
    return jnp.maximum(z2, 0.0) + jnp.log(1.0 + jnp.exp2(-jnp.abs(z2))) * LOG2E


def _lane_tiles(x, width):
    return jnp.concatenate([x] * (width // x.shape[-1]), axis=-1) if width != x.shape[-1] else x


def _strict_lower_tri(n):
    r = lax.broadcasted_iota(jnp.int32, (n, n), 0)
    c = lax.broadcasted_iota(jnp.int32, (n, n), 1)
    return (r > c).astype(BF16)


def _layer_norm(v, g, b):
    mu = jnp.mean(v, axis=-1, keepdims=True)
    d = v - mu
    var = jnp.mean(d * d, axis=-1, keepdims=True)
    return d * lax.rsqrt(var + LN_EPS) * g + b


def _dot(a, b):
    return jnp.dot(a, b, preferred_element_type=F32)


def _dot_nt(a, b):
    return lax.dot_general(a, b, (((1,), (1,)), ((), ())), preferred_element_type=F32)


def _ada_kernel(c_ref, w_ref, b_ref, o_ref):
    c = c_ref[...]
    s = (c * _sigmoid(c)).astype(BF16)
    o_ref[...] = _dot(s, w_ref[...].astype(BF16)) + b_ref[...]


def _ada_call(c, w, b):
    n_sub, d, d3 = w.shape
    tn = d
    rows = c.shape[0]
    return pl.pallas_call(
        _ada_kernel,
        grid=(n_sub, d3 // tn),
        in_specs=[
            pl.BlockSpec((rows, d), lambda i, j: (0, 0)),
            pl.BlockSpec((None, d, tn), lambda i, j: (i, 0, j)),
            pl.BlockSpec((None, 1, tn), lambda i, j: (i, 0, j)),
        ],
        out_specs=pl.BlockSpec((None, rows, tn), lambda i, j: (i, 0, j)),
        out_shape=jax.ShapeDtypeStruct((n_sub, rows, d3), F32),
        compiler_params=_cparams(("parallel", "parallel")),
        name="ada_mod",
    )(c, w, b)


def _mod_spec(mod, tm, rows):
    g, r, d3 = mod.shape
    tiles_per_group = rows // tm // g
    return pl.BlockSpec((None, r, d3), lambda i, *_: (i // tiles_per_group, 0, 0))


def _mlp_kernel(x_ref, mod_ref, w1_ref, w2_ref, g_ref, b_ref, o_ref, h_sc, acc_sc, *, alpha):
    f = pl.program_id(1)
    d = x_ref.shape[-1]

    @pl.when(f == 0)
    def _():
        mod = mod_ref[...]
        h_sc[...] = (x_ref[...] * (1.0 + mod[:, d:2 * d]) + mod[:, :d]).astype(BF16)

    a = jnp.maximum(_dot(h_sc[...], w1_ref[...]), 0.0)
    y = _dot((a * a).astype(BF16), w2_ref[...])

    @pl.when(f == 0)
    def _():
        acc_sc[...] = y

    @pl.when(f > 0)
    def _():
        acc_sc[...] += y

    @pl.when(f == pl.num_programs(1) - 1)
    def _():
        gate = mod_ref[...][:, 2 * d:]
        v = alpha * x_ref[...] + (1.0 + gate) * acc_sc[...]
        o_ref[...] = _layer_norm(v, g_ref[...], b_ref[...])


def _mlp_call(x, mod, w1, w2, g, b, *, alpha, tm, tf):
    rows, d = x.shape
    dff = w1.shape[1]
    return pl.pallas_call(
        functools.partial(_mlp_kernel, alpha=alpha),
        grid=(rows // tm, dff // tf),
        in_specs=[
            pl.BlockSpec((tm, d), lambda i, f: (i, 0)),
            _mod_spec(mod, tm, rows),
            pl.BlockSpec((d, tf), lambda i, f: (0, f)),
            pl.BlockSpec((tf, d), lambda i, f: (f, 0)),
            pl.BlockSpec((1, d), lambda i, f: (0, 0)),
            pl.BlockSpec((1, d), lambda i, f: (0, 0)),
        ],
        out_specs=pl.BlockSpec((tm, d), lambda i, f: (i, 0)),
        out_shape=jax.ShapeDtypeStruct((rows, d), F32),
        scratch_shapes=[pltpu.VMEM((tm, d), BF16), pltpu.VMEM((tm, d), F32)],
        compiler_params=_cparams(("parallel", "arbitrary")),
        name="mlp_sublayer",
    )(x, mod, w1, w2, g, b)


def _kv_kernel(x_ref, w_ref, k_ref, v_ref, kb_ref, vb_ref):
    d = x_ref.shape[-1]
    kv = _dot(x_ref[...].astype(BF16), w_ref[...])
    k = kv[:, :d]
    v = kv[:, d:]
    k_ref[...] = k
    v_ref[...] = v
    kb_ref[...] = k.astype(BF16)
    vb_ref[...] = v.astype(BF16)


def _kv_call(x, w, *, tm):
    rows, d = x.shape
    row_spec = pl.BlockSpec((tm, d), lambda i: (i, 0))
    return pl.pallas_call(
        _kv_kernel,
        grid=(rows // tm,),
        in_specs=[row_spec, pl.BlockSpec((d, 2 * d), lambda i: (0, 0))],
        out_specs=[row_spec] * 4,
        out_shape=[jax.ShapeDtypeStruct((rows, d), F32)] * 2 + [jax.ShapeDtypeStruct((rows, d), BF16)] * 2,
        compiler_params=_cparams(("parallel",)),
        name="kv_proj",
    )(x, w)


def _q_kernel(x_ref, mod_ref, w_ref, q_ref, *, qscale):
    d = x_ref.shape[-1]
    mod = mod_ref[...]
    h = (x_ref[...] * (1.0 + mod[:, d:2 * d]) + mod[:, :d]).astype(BF16)
    q_ref[...] = (_dot(h, w_ref[...]) * qscale).astype(BF16)


def _q_call(x, mod, w, *, qscale, tm):
    rows, d = x.shape
    row_spec = pl.BlockSpec((tm, d), lambda i: (i, 0))
    return pl.pallas_call(
        functools.partial(_q_kernel, qscale=qscale),
        grid=(rows // tm,),
        in_specs=[row_spec, _mod_spec(mod, tm, rows), pl.BlockSpec((d, d), lambda i: (0, 0))],
        out_specs=row_spec,
        out_shape=jax.ShapeDtypeStruct((rows, d), BF16),
        compiler_params=_cparams(("parallel",)),
        name="q_proj",
    )(x, mod, w)


def _oproj_kernel(o_ref, x_ref, mod_ref, w_ref, g_ref, b_ref, y_ref, *, alpha):
    d = x_ref.shape[-1]
    gate = mod_ref[...][:, 2 * d:]
    y = _dot(o_ref[...], w_ref[...])
    v = alpha * x_ref[...] + (1.0 + gate) * y
    y_ref[...] = _layer_norm(v, g_ref[...], b_ref[...])


def _oproj_call(o, x, mod, w, g, b, *, alpha, tm):
    rows, d = x.shape
    row_spec = pl.BlockSpec((tm, d), lambda i: (i, 0))
    vec_spec = pl.BlockSpec((1, d), lambda i: (0, 0))
    return pl.pallas_call(
        functools.partial(_oproj_kernel, alpha=alpha),
        grid=(rows // tm,),
        in_specs=[row_spec, row_spec, _mod_spec(mod, tm, rows), pl.BlockSpec((d, d), lambda i: (0, 0)),
                  vec_spec, vec_spec],
        out_specs=row_spec,
        out_shape=jax.ShapeDtypeStruct((rows, d), F32),
        compiler_params=_cparams(("parallel",)),
        name="attn_out_proj",
    )(o, x, mod, w, g, b)


CONV_ROW_CHUNK = 64


def _conv_prompt_kernel(x_ref, mod_ref, pw1_ref, pw1b_ref, dww_ref, dwb_ref, cg_ref, cb_ref, pw2_ref,
                        g_ref, b_ref, o_ref, st_ref, ubuf, sbuf, ybuf, *, alpha, width, halo):
    t = pl.program_id(1)
    tm, d = x_ref.shape
    sub = V7X_SUBLANES

    @pl.when(t == 0)
    def _():
        ubuf[0:halo, :] = jnp.zeros((halo, d), F32)

    @pl.when(t > 0)
    def _():
        ubuf[0:halo, :] = ubuf[tm:tm + halo, :]

    x = x_ref[...]
    mod = mod_ref[...]
    h = (x * (1.0 + mod[:, d:2 * d]) + mod[:, :d]).astype(BF16)
    ag = _dot(h, pw1_ref[...]) + pw1b_ref[...]
    ubuf[halo:halo + tm, :] = ag[:, :d] * _sigmoid(ag[:, d:])

    base = halo - (width - 1)
    srows = sbuf.shape[1]
    for c0 in range(0, d, V7X_LANES):
        cols = slice(c0, c0 + V7X_LANES)
        xcol = ubuf[:, cols]
        for r in range(1, sub):
            sbuf[r - 1, :, cols] = pltpu.roll(xcol, xcol.shape[0] - r, axis=0)[0:srows]
    for r0 in range(0, tm, CONV_ROW_CHUNK):
        for c0 in range(0, d, V7X_LANES):
            cols = slice(c0, c0 + V7X_LANES)
            acc = jnp.broadcast_to(dwb_ref[:, cols], (CONV_ROW_CHUNK, V7X_LANES))
            for j in range(width):
                a, r = divmod(base + j, sub)
                rows = pl.ds(r0 + sub * a, CONV_ROW_CHUNK)
                src = ubuf[rows, cols] if r == 0 else sbuf[r - 1, rows, cols]
                acc = acc + src * dww_ref[j:j + 1, cols]
            ybuf[r0:r0 + CONV_ROW_CHUNK, cols] = acc

    y = _layer_norm(ybuf[...], cg_ref[...], cb_ref[...])
    y = (y * _sigmoid(y)).astype(BF16)
    y2 = _dot(y, pw2_ref[...])
    v = alpha * x + (1.0 + mod[:, 2 * d:]) * y2
    o_ref[...] = _layer_norm(v, g_ref[...], b_ref[...])

    @pl.when(t == pl.num_programs(1) - 1)
    def _():
        st_ref[...] = ubuf[tm:tm + halo, :]


def _conv_prompt_call(x, mod, pw1, pw1b, dww, dwb, cg, cb, pw2, g, b, *, alpha, batch, tm):
    rows, d = x.shape
    width = dww.shape[0]
    halo = -(-(width - 1) // V7X_SUBLANES) * V7X_SUBLANES
    nt = rows // batch // tm
    vec = lambda n: pl.BlockSpec((1, n), lambda bi, t: (0, 0))
    out, st = pl.pallas_call(
        functools.partial(_conv_prompt_kernel, alpha=alpha, width=width, halo=halo),
        grid=(batch, nt),
        in_specs=[
            pl.BlockSpec((tm, d), lambda bi, t: (bi * nt + t, 0)),
            pl.BlockSpec((None, 1, 3 * d), lambda bi, t: (bi, 0, 0)),
            pl.BlockSpec((d, 2 * d), lambda bi, t: (0, 0)),
            vec(2 * d),
            pl.BlockSpec((width, d), lambda bi, t: (0, 0)),
            vec(d), vec(d), vec(d),
            pl.BlockSpec((d, d), lambda bi, t: (0, 0)),
            vec(d), vec(d),
        ],
        out_specs=[
            pl.BlockSpec((tm, d), lambda bi, t: (bi * nt + t, 0)),
            pl.BlockSpec((None, halo, d), lambda bi, t: (bi, 0, 0)),
        ],
        out_shape=[jax.ShapeDtypeStruct((rows, d), F32), jax.ShapeDtypeStruct((batch, halo, d), F32)],
        scratch_shapes=[
            pltpu.VMEM((tm + halo, d), F32),
            pltpu.VMEM((V7X_SUBLANES - 1, tm + halo - V7X_SUBLANES, d), F32),
            pltpu.VMEM((tm, d), F32),
        ],
        compiler_params=_cparams(("arbitrary", "arbitrary")),
        name="conformer_conv_prompt",
    )(x, mod, pw1, pw1b, dww, dwb, cg, cb, pw2, g, b)
    return out, st[:, halo - (width - 1):, :]


CONV_LANE_CHUNK = 512


def _conv_sample_kernel(x_ref, mod_ref, st_ref, pw1_ref, pw1b_ref, dww_ref, dwb_ref, cg_ref, cb_ref, pw2_ref,
                        g_ref, b_ref, o_ref, nst_ref, ubuf, ybuf, *, alpha):
    rows, d = x_ref.shape
    hist, db, _ = st_ref.shape
    ts = rows // db
    width = hist + 1

    x = x_ref[...]
    mod = mod_ref[...]
    h = (x * (1.0 + mod[:, d:2 * d]) + mod[:, :d]).astype(BF16)
    ag = _dot(h, pw1_ref[...]) + pw1b_ref[...]
    ubuf[...] = ag[:, :d] * _sigmoid(ag[:, d:])

    def full_row(k, c0):
        if k < hist:
            return st_ref[k, :, c0:c0 + CONV_LANE_CHUNK]
        return ubuf[(k - hist) * db:(k - hist + 1) * db, c0:c0 + CONV_LANE_CHUNK]

    for t in range(ts):
        for c0 in range(0, d, CONV_LANE_CHUNK):
            acc = jnp.broadcast_to(dwb_ref[:, c0:c0 + CONV_LANE_CHUNK], (db, CONV_LANE_CHUNK))
            for j in range(width):
                acc = acc + full_row(t + j, c0) * dww_ref[j:j + 1, c0:c0 + CONV_LANE_CHUNK]
            ybuf[t * db:(t + 1) * db, c0:c0 + CONV_LANE_CHUNK] = acc

    y = _layer_norm(ybuf[...], cg_ref[...], cb_ref[...])
    y = (y * _sigmoid(y)).astype(BF16)
    y2 = _dot(y, pw2_ref[...])
    v = alpha * x + (1.0 + mod[:, 2 * d:]) * y2
    o_ref[...] = _layer_norm(v, g_ref[...], b_ref[...])

    for k in range(hist):
        src = k + ts
        if src < hist:
            nst_ref[k] = st_ref[src]
        else:
            nst_ref[k] = ubuf[(src - hist) * db:(src - hist + 1) * db, :]


def _conv_sample_call(x, mod, st, pw1, pw1b, dww, dwb, cg, cb, pw2, g, b, *, alpha):
    rows, d = x.shape
    full = lambda a: pl.BlockSpec(a.shape, lambda i, _n=a.ndim: (0,) * _n)
    args = (x, mod, st, pw1, pw1b, dww, dwb, cg, cb, pw2, g, b)
    return pl.pallas_call(
        functools.partial(_conv_sample_kernel, alpha=alpha),
        grid=(1,),
        in_specs=[full(a) for a in args],
        out_specs=[pl.BlockSpec((rows, d), lambda i: (0, 0)), pl.BlockSpec(st.shape, lambda i: (0, 0, 0))],
        out_shape=[jax.ShapeDtypeStruct((rows, d), F32), jax.ShapeDtypeStruct(st.shape, F32)],
        scratch_shapes=[pltpu.VMEM((rows, d), F32), pltpu.VMEM((rows, d), F32)],
        compiler_params=_cparams(("arbitrary",)),
        name="conformer_conv_sample",
    )(*args)


ATTN_ROW_CHUNK = 32


MASKED_LOG2 = -1e30


def _attn_prompt_kernel(bias_ref, q_ref, k_ref, v_ref, tri_ref, o_ref, qs_sc, z_sc, sp_sc, u_sc, cs_sc, w_sc,
                        acc_sc, r_sc, *, tq, hd):
    p = pl.program_id(1)
    i = pl.program_id(2)
    lanes = q_ref.shape[-1]
    heads = lanes // hd
    rows = heads * tq
    rc = ATTN_ROW_CHUNK

    q = q_ref[...].astype(F32)
    lane = lax.broadcasted_iota(jnp.int32, (1, lanes), 1)
    for a in range(heads):
        qs_sc[a * tq:(a + 1) * tq, :] = jnp.where(lane // hd == a, q, 0.0).astype(BF16)
    acc_sc[...] = jnp.zeros(acc_sc.shape, F32)
    r_sc[...] = jnp.zeros(r_sc.shape, F32)
    bias = [bias_ref[p * heads + a] for a in range(heads)]

    def logits(kb, slot):
        kblk = k_ref[pl.ds(pl.multiple_of(kb * tq, tq), tq), :]
        z_sc[slot] = _dot_nt(qs_sc[...], kblk)

    def scores(slot, masked):
        for r0 in range(0, rows, rc):
            rs = pl.ds(r0, rc)
            z = z_sc[slot, rs, :] + bias[r0 // tq]
            sp = _softplus2(z)
            rsum = r_sc[rs, :]
            u = (z - sp) - _lane_tiles(rsum, tq)
            if masked:
                qpos = (r0 % tq) + lax.broadcasted_iota(jnp.int32, (rc, tq), 0)
                vis = lax.broadcasted_iota(jnp.int32, (rc, tq), 1) < qpos
                sp = jnp.where(vis, sp, 0.0)
                u = jnp.where(vis, u, MASKED_LOG2)
            sp_sc[rs, :] = sp.astype(BF16)
            u_sc[slot, rs, :] = u
            r_sc[rs, :] = rsum + jnp.broadcast_to(jnp.sum(sp, axis=-1, keepdims=True), rsum.shape)
        cs_sc[slot] = _dot(sp_sc[...], tri_ref[...])

    def weights(kb, slot):
        vblk = v_ref[pl.ds(pl.multiple_of(kb * tq, tq), tq), :]
        for r0 in range(0, rows, rc):
            rs = pl.ds(r0, rc)
            w_sc[rs, :] = jnp.exp2(u_sc[slot, rs, :] - cs_sc[slot, rs, :]).astype(BF16)
        acc_sc[...] += _dot(w_sc[...], vblk)

    logits(i, 0)
    scores(0, True)
    logits(jnp.maximum(i - 1, 0), 1)

    def body(n, carry):
        weights(i - n + 1, (n - 1) % 2)
        scores(n % 2, False)
        logits(jnp.maximum(i - n - 1, 0), (n + 1) % 2)
        return carry

    lax.fori_loop(1, i + 1, body, 0)
    weights(0, i % 2)

    out = acc_sc[0:tq, :]
    for a in range(1, heads):
        out = jnp.where(lane >= a * hd, acc_sc[a * tq:(a + 1) * tq, :], out)
    o_ref[...] = out.astype(o_ref.dtype)


def _attn_prompt_call(q, k, v, bias2, *, batch, hd, tq):
    rows, d = q.shape
    t = rows // batch
    nq = t // tq
    ncol = d // V7X_LANES
    srows = (V7X_LANES // hd) * tq
    return pl.pallas_call(
        functools.partial(_attn_prompt_kernel, tq=tq, hd=hd),
        grid=(batch, ncol, nq),
        in_specs=[
            pl.BlockSpec(memory_space=pltpu.SMEM),
            pl.BlockSpec((tq, V7X_LANES), lambda b, p, i: (b * nq + i, p)),
            pl.BlockSpec((t, V7X_LANES), lambda b, p, i: (b, p)),
            pl.BlockSpec((t, V7X_LANES), lambda b, p, i: (b, p)),
            pl.BlockSpec((tq, tq), lambda b, p, i: (0, 0)),
        ],
        out_specs=pl.BlockSpec((tq, V7X_LANES), lambda b, p, i: (b * nq + i, p)),
        out_shape=jax.ShapeDtypeStruct((rows, d), BF16),
        scratch_shapes=[
            pltpu.VMEM((srows, V7X_LANES), BF16),
            pltpu.VMEM((2, srows, tq), F32),
            pltpu.VMEM((srows, tq), BF16),
            pltpu.VMEM((2, srows, tq), F32),
            pltpu.VMEM((2, srows, tq), F32),
            pltpu.VMEM((srows, tq), BF16),
            pltpu.VMEM((srows, V7X_LANES), F32),
            pltpu.VMEM((srows, V7X_LANES), F32),
        ],
        compiler_params=_cparams(("parallel", "parallel", "parallel")),
        name="stickbreak_attn_prompt",
    )(bias2, q, k, v, _strict_lower_tri(tq))


def _attn_sample_kernel(pt_ref, q_ref, kn_ref, vn_ref, bias_ref, tri_ref, *refs, pps, n_heads):
    del pt_ref
    k_refs = refs[:pps]
    v_refs = refs[pps:2 * pps]
    o_ref, wt_sc, acc_sc, r_sc = refs[2 * pps:]
    s = pl.program_id(1)
    ts, d = q_ref.shape
    hd = d // n_heads
    nr = ts * n_heads
    span = tri_ref.shape[0]

    row_head = lax.broadcasted_iota(jnp.int32, (n_heads, d), 0)
    lane_head = lax.broadcasted_iota(jnp.int32, (n_heads, d), 1) // hd
    own = row_head == lane_head

    @pl.when(s == 0)
    def _():
        q = q_ref[...].astype(F32)
        for t in range(ts):
            wt_sc[t * n_heads:(t + 1) * n_heads, :] = jnp.where(
                own, jnp.broadcast_to(q[t:t + 1, :], (n_heads, d)), 0.0).astype(BF16)
        wt = wt_sc[...].astype(F32)
        kn = kn_ref[...]
        vn = vn_ref[...]
        bias_col = bias_ref[:, 0:1]
        row_t = lax.broadcasted_iota(jnp.int32, (nr, 1), 0) // n_heads
        zs, sps, spms = [], [], []
        for j in range(ts):
            z = jnp.sum(wt * kn[j:j + 1, :], axis=-1, keepdims=True) + bias_col
            sp = _softplus2(z)
            zs.append(z)
            sps.append(sp)
            spms.append(jnp.where(j < row_t, sp, 0.0))
        acc = jnp.zeros((nr, d), F32)
        after = jnp.zeros((nr, 1), F32)
        for j in reversed(range(ts)):
            w = jnp.where(j < row_t, jnp.exp2(zs[j] - sps[j] - after), 0.0)
            acc = acc + w * vn[j:j + 1, :]
            after = after + spms[j]
        acc_sc[...] = acc
        r_sc[...] = jnp.broadcast_to(after, r_sc.shape)

    wt = wt_sc[...]
    bias = _lane_tiles(bias_ref[...], span)
    tri = tri_ref[...]
    acc = acc_sc[...]
    rsum = r_sc[...]
    per = span // k_refs[0].shape[-1]
    for i in range(0, pps, per):
        kt = jnp.concatenate([k_refs[i + c][...] for c in reversed(range(per))], axis=-1).astype(BF16)
        vt = jnp.concatenate([v_refs[i + c][...] for c in reversed(range(per))], axis=-1).astype(BF16)
        z = _dot(wt, kt) + bias
        sp = _softplus2(z)
        cs = _dot(sp.astype(BF16), tri)
        w = jnp.exp2(z - sp - cs - _lane_tiles(rsum, span))
        acc = acc + _dot_nt(w.astype(BF16), vt)
        rsum = rsum + jnp.broadcast_to(jnp.sum(sp, axis=-1, keepdims=True), rsum.shape)
    acc_sc[...] = acc
    r_sc[...] = rsum

    @pl.when(s == pl.num_programs(1) - 1)
    def _():
        for t in range(ts):
            blk = jnp.where(own, acc[t * n_heads:(t + 1) * n_heads, :], 0.0)
            o_ref[t:t + 1, :] = jnp.sum(blk, axis=0, keepdims=True).astype(o_ref.dtype)


def _attn_sample_call(q, kn, vn, ckt, cvt, page_table, bias2, *, pps, span):
    db, ts, d = q.shape
    n_pool, _, page = ckt.shape
    n_heads = bias2.shape[0]
    n_pages = page_table.shape[1]
    n_steps = n_pages // pps
    nr = ts * n_heads
    bias_rows = jnp.broadcast_to(jnp.tile(bias2, ts)[:, None], (nr, V7X_LANES)).astype(F32)

    def page_spec(i):
        return pl.BlockSpec((None, d, page), lambda b, s, pt: (pt[b, n_pages - 1 - (s * pps + i)], 0, 0))

    seq_spec = pl.BlockSpec((None, ts, d), lambda b, s, pt: (b, 0, 0))
    grid_spec = pltpu.PrefetchScalarGridSpec(
        num_scalar_prefetch=1,
        grid=(db, n_steps),
        in_specs=[seq_spec, seq_spec, seq_spec, pl.BlockSpec((nr, V7X_LANES), lambda b, s, pt: (0, 0)),
                  pl.BlockSpec((span, span), lambda b, s, pt: (0, 0))]
        + [page_spec(i) for i in range(pps)] * 2,
        out_specs=seq_spec,
        scratch_shapes=[pltpu.VMEM((nr, d), BF16), pltpu.VMEM((nr, d), F32), pltpu.VMEM((nr, V7X_LANES), F32)],
    )
    return pl.pallas_call(
        functools.partial(_attn_sample_kernel, pps=pps, n_heads=n_heads),
        grid_spec=grid_spec,
        out_shape=jax.ShapeDtypeStruct((db, ts, d), BF16),
        compiler_params=_cparams(("parallel", "arbitrary")),
        name="stickbreak_attn_sample",
    )(page_table, q, kn, vn, bias_rows, _strict_lower_tri(span), *([ckt] * pps), *([cvt] * pps))


def _row_tile(rows, want):
    tm = min(rows, want)
    assert rows % tm == 0, (rows, tm)
    return tm


def kernel(x_prompt, x_sample, c_prompt, c_sample, state_conv, cache_k, cache_v, page_table, ada_w, ada_b, ln_g,
           ln_b, mlp_w1, mlp_w2, conv_pw1_w, conv_pw1_b, conv_dw_w, conv_dw_b, conv_ln_g, conv_ln_b, conv_pw2_w,
           w_kv, attn_wq, attn_wo, attn_bias):
    bsz, seq, d = x_prompt.shape
    db, ts, _ = x_sample.shape
    depth = ada_w.shape[0]
    n_a = conv_pw1_w.shape[0]
    n_heads = attn_bias.shape[1]
    hd = d // n_heads
    alpha = (2.0 * depth) ** 0.25
    qscale = hd ** -0.5 * LOG2E
    bias2 = attn_bias * LOG2E
    n_pool, page = cache_k.shape[:2]
    ckt = cache_k.transpose(0, 2, 3, 1).reshape(n_pool, d, page)
    cvt = cache_v.transpose(0, 2, 3, 1).reshape(n_pool, d, page)

    w1 = mlp_w1.astype(BF16)
    w2 = mlp_w2.astype(BF16)
    pw1 = conv_pw1_w.astype(BF16)
    pw2 = conv_pw2_w.astype(BF16)
    wkv = w_kv.astype(BF16)
    wq = attn_wq.astype(BF16)
    wo = attn_wo.astype(BF16)

    c_all = jnp.concatenate([c_prompt, c_sample], axis=0)
    mods = _ada_call(c_all, ada_w.reshape(2 * depth, d, 3 * d), ada_b.reshape(2 * depth, 1, 3 * d))
    mod_p = mods[:, :bsz, None, :]
    mod_s = mods[:, bsz:, :]
    mod_s_tm = jnp.tile(mod_s, (1, ts, 1))[:, None]
    mod_s_bm = jnp.repeat(mod_s, ts, axis=1)[:, None]

    rows_p = bsz * seq
    rows_s = db * ts
    tm_p = _row_tile(seq, 512)
    vec = lambda a: a.reshape(1, -1)

    def trunk_tail(x, mod, l, rows, tm):
        return _mlp_call(x, mod[2 * l + 1], w1[l], w2[l], vec(ln_g[l, 1]), vec(ln_b[l, 1]),
                         alpha=alpha, tm=tm, tf=_row_tile(w1.shape[2], 1024))

    xp = x_prompt.reshape(rows_p, d)
    conv_new_p = []
    for l in range(depth):
        if l < n_a:
            xp, st = _conv_prompt_call(
                xp, mod_p[2 * l], pw1[l], vec(conv_pw1_b[l]), conv_dw_w[l], vec(conv_dw_b[l]),
                vec(conv_ln_g[l]), vec(conv_ln_b[l]), pw2[l], vec(ln_g[l, 0]), vec(ln_b[l, 0]),
                alpha=alpha, batch=bsz, tm=_row_tile(seq, 256))
            conv_new_p.append(st)
        else:
            j = l - n_a
            if l == n_a:
                kp_f, vp_f, kp_b, vp_b = _kv_call(xp, wkv, tm=tm_p)
            q = _q_call(xp, mod_p[2 * l], wq[j], qscale=qscale, tm=tm_p)
            o = _attn_prompt_call(q, kp_b, vp_b, bias2[j], batch=bsz, hd=hd, tq=_row_tile(seq, V7X_MXU_DIM))
            xp = _oproj_call(o, xp, mod_p[2 * l], wo[j], vec(ln_g[l, 0]), vec(ln_b[l, 0]), alpha=alpha, tm=tm_p)
        xp = trunk_tail(xp, mod_p, l, rows_p, tm_p)

    assert n_a >= 1, "the sample group enters its conv layers in time-major row order"
    xs = x_sample.transpose(1, 0, 2).reshape(rows_s, d)
    conv_new_s = []
    for l in range(depth):
        if l < n_a:
            xs, st = _conv_sample_call(
                xs, mod_s_tm[2 * l, 0], state_conv[l].transpose(1, 0, 2), pw1[l], vec(conv_pw1_b[l]), conv_dw_w[l],
                vec(conv_dw_b[l]), vec(conv_ln_g[l]), vec(conv_ln_b[l]), pw2[l], vec(ln_g[l, 0]), vec(ln_b[l, 0]),
                alpha=alpha)
            conv_new_s.append(st.transpose(1, 0, 2))
            xs = trunk_tail(xs, mod_s_tm, l, rows_s, rows_s)
            if l == n_a - 1:
                xs = xs.reshape(ts, db, d).transpose(1, 0, 2).reshape(rows_s, d)
        else:
            j = l - n_a
            if l == n_a:
                ks_f, vs_f, _, _ = _kv_call(xs, wkv, tm=rows_s)
            q = _q_call(xs, mod_s_bm[2 * l], wq[j], qscale=qscale, tm=rows_s)
            o = _attn_sample_call(q.reshape(db, ts, d), ks_f.reshape(db, ts, d), vs_f.reshape(db, ts, d),
                                  ckt, cvt, page_table, bias2[j], pps=8, span=2 * page)
            xs = _oproj_call(o.reshape(rows_s, d), xs, mod_s_bm[2 * l], wo[j], vec(ln_g[l, 0]), vec(ln_b[l, 0]),
                             alpha=alpha, tm=rows_s)
            xs = trunk_tail(xs, mod_s_bm, l, rows_s, rows_s)

    return (xp.reshape(bsz, seq, d), xs.reshape(db, ts, d),
            jnp.stack(conv_new_p, axis=0), jnp.stack(conv_new_s, axis=0),
            kp_f.reshape(bsz, seq, n_heads, hd), vp_f.reshape(bsz, seq, n_heads, hd),
            ks_f.reshape(db, ts, n_heads, hd), vs_f.reshape(db, ts, n_heads, hd))
```
